```python
import math
import jax, jax.numpy as jnp
from jax import lax
import numpy as np

D_MODEL = 4096
BATCH = 8
SEQ = 2048
DEPTH = 1
DEC_BATCH = 2
DEC_SEQ = 4096
PAST_LEN = 128

MLSTM_HEADS = 4
MLSTM_V = (D_MODEL // 2) // MLSTM_HEADS
MLSTM_QK = MLSTM_V // 2
MLSTM_V_W = MLSTM_HEADS * MLSTM_V
MLSTM_QK_W = MLSTM_HEADS * MLSTM_QK
MLSTM_GATES = 4 * MLSTM_HEADS
CHUNK = 64
FORGET_BIAS_INIT = 3.0

QK_NOPE = 128
QK_ROPE = 64
V_HEAD = 128
MLA_HEADS = (D_MODEL // 2) // V_HEAD
Q_LORA = D_MODEL // 4
KV_LORA = 512
ROPE_THETA = 10000.0
Q_BLOCK = 128

MIX_WIDTH = MLSTM_V_W + MLA_HEADS * V_HEAD
IN_SIZES = (MLSTM_QK_W, MLSTM_QK_W, MLSTM_V_W, MLSTM_V_W, MLSTM_GATES, Q_LORA, KV_LORA, QK_ROPE)
IN_COLS = sum(IN_SIZES)

D_FF = ((8 * D_MODEL // 3 + 255) // 256) * 256
EPS = 1e-6

kernel_name = "hybrid_mlstm_mla_encoder"


def rms_norm(x, g):
    xf = x.astype(jnp.float32)
    y = xf * lax.rsqrt(jnp.mean(xf * xf, axis=-1, keepdims=True) + EPS)
    return (y * g.astype(jnp.float32)).astype(x.dtype)


def rope_tables(S, dtype):
    inv = 1.0 / (ROPE_THETA ** (jnp.arange(0, QK_ROPE, 2, dtype=jnp.float32) / QK_ROPE))
    ang = jnp.arange(S, dtype=jnp.float32)[:, None] * inv[None, :]
    return jnp.cos(ang).astype(dtype), jnp.sin(ang).astype(dtype)


def apply_rope(x, cos, sin):
    x1, x2 = jnp.split(x, 2, axis=-1)
    return jnp.concatenate([x1 * cos - x2 * sin, x1 * sin + x2 * cos], axis=-1)


def mlstm_scan(q, k, v, i_pre, f_pre):
    B, H, S, dk = q.shape
    dv = v.shape[-1]
    nc = S // CHUNK

    def chunks(a):
        return jnp.moveaxis(a.reshape(B, H, nc, CHUNK, *a.shape[3:]), 2, 0)

    logf = jax.nn.log_sigmoid(f_pre)
    xs = (chunks(q), chunks(k), chunks(v), chunks(i_pre), chunks(logf))
    mask = jnp.tril(jnp.ones((CHUNK, CHUNK), dtype=bool))

    def step(carry, inp):
        C, n, m = carry
        qc, kc, vc, ic, lfc = inp
        b = jnp.cumsum(lfc, axis=-1)
        Dm = jnp.where(mask, b[..., :, None] - b[..., None, :] + ic[..., None, :], -jnp.inf)
        inter = b + m[..., None]
        mt = jnp.maximum(inter, jnp.max(Dm, axis=-1))
        a_inter = jnp.exp(inter - mt)
        s = jnp.einsum('bhtd,bhsd->bhts', qc, kc) * jnp.exp(Dm - mt[..., None])
        num = jnp.einsum('bhts,bhsv->bhtv', s, vc) + a_inter[..., None] * jnp.einsum('bhvd,bhtd->bhtv', C, qc)
        den = jnp.sum(s, axis=-1) + a_inter * jnp.einsum('bhd,bhtd->bht', n, qc)
        h = num / jnp.maximum(jnp.abs(den), jnp.exp(-mt))[..., None]
        bL = b[..., -1]
        g = bL[..., None] - b + ic
        m_new = jnp.maximum(bL + m, jnp.max(g, axis=-1))
        decay = jnp.exp(bL + m - m_new)
        w = jnp.exp(g - m_new[..., None])
        C_new = decay[..., None, None] * C + jnp.einsum('bhsv,bhsd->bhvd', vc * w[..., None], kc)
        n_new = decay[..., None] * n + jnp.einsum('bhs,bhsd->bhd', w, kc)
        return (C_new, n_new, m_new), h

    init = (jnp.zeros((B, H, dv, dk), jnp.float32), jnp.zeros((B, H, dk), jnp.float32),
            jnp.zeros((B, H), jnp.float32))
    _, hs = lax.scan(step, init, xs)
    return jnp.moveaxis(hs, 0, 2).reshape(B, H, S, dv)


def token_mixer(h, w_in, gate_bias, mlstm_norm, q_norm, kv_norm, w_uq, w_ukv, w_out):
    B, S, _ = h.shape
    proj = h @ w_in
    idx = [int(c) for c in np.cumsum(IN_SIZES)[:-1]]
    q_m, k_m, v_m, o_m, gates, q_dn, kv_dn, k_r = jnp.split(proj, idx, axis=-1)

    def heads(a, d):
        return a.reshape(B, S, MLSTM_HEADS, d).transpose(0, 2, 1, 3).astype(jnp.float32)

    qh = heads(q_m, MLSTM_QK) * (MLSTM_QK ** -0.5)
    kh = heads(k_m, MLSTM_QK)
    vh = heads(v_m, MLSTM_V)
    g = (gates.astype(jnp.float32) + gate_bias.astype(jnp.float32))
    g = g.reshape(B, S, 4, MLSTM_HEADS).transpose(2, 0, 3, 1)
    flip = lambda a: jnp.flip(a, axis=2)
    h_fwd = mlstm_scan(qh, kh, vh, g[0], g[1])
    h_bwd = flip(mlstm_scan(flip(qh), flip(kh), flip(vh), flip(g[2]), flip(g[3])))
    hm = rms_norm(h_fwd + h_bwd, mlstm_norm.reshape(MLSTM_HEADS, 1, MLSTM_V))
    hm = hm.transpose(0, 2, 1, 3).reshape(B, S, MLSTM_V_W).astype(h.dtype)
    y_mlstm = jax.nn.sigmoid(o_m) * hm

    cq = rms_norm(q_dn, q_norm)
    qf = (cq @ w_uq).reshape(B, S, MLA_HEADS, QK_NOPE + QK_ROPE)
    q_nope, q_rope = qf[..., :QK_NOPE], qf[..., QK_NOPE:]
    ckv = rms_norm(kv_dn, kv_norm)
    kvf = (ckv @ w_ukv).reshape(B, S, MLA_HEADS, QK_NOPE + V_HEAD)
    k_nope, v = kvf[..., :QK_NOPE], kvf[..., QK_NOPE:]
    cos, sin = rope_tables(S, h.dtype)
    q_rope = apply_rope(q_rope, cos[:, None, :], sin[:, None, :])
    k_rope = apply_rope(k_r, cos, sin)
    scale = (QK_NOPE + QK_ROPE) ** -0.5
    nq = S // Q_BLOCK

    def blocks(a):
        return a.reshape(B, nq, Q_BLOCK, *a.shape[2:]).swapaxes(0, 1)

    def attend(blk):
        qn, qr = blk
        s = (jnp.einsum('bqhd,bkhd->bhqk', qn, k_nope).astype(jnp.float32)
             + jnp.einsum('bqhr,bkr->bhqk', qr, k_rope).astype(jnp.float32)) * scale
        p = jax.nn.softmax(s, axis=-1).astype(v.dtype)
        return jnp.einsum('bhqk,bkhd->bqhd', p, v)

    o = lax.map(attend, (blocks(q_nope), blocks(q_rope)))
    y_mla = o.swapaxes(0, 1).reshape(B, S, MLA_HEADS * V_HEAD)

    return jnp.concatenate([y_mlstm, y_mla], axis=-1) @ w_out


def setup_inputs(seed: int = 0) -> dict:
    key = jax.random.key(seed)
    ks = jax.random.split(key, 20)
    f32 = jnp.float32

    def w(k, shape, fan_in):
        return jax.random.normal(k, shape, f32) * (fan_in ** -0.5)

    def gain(k, n):
        return jnp.ones((DEPTH, n), f32) + 0.02 * jax.random.normal(k, (DEPTH, n), f32)

    base = jnp.concatenate([jnp.zeros((MLSTM_HEADS,), f32), jnp.full((MLSTM_HEADS,), FORGET_BIAS_INIT, f32),
                            jnp.zeros((MLSTM_HEADS,), f32), jnp.full((MLSTM_HEADS,), FORGET_BIAS_INIT, f32)])
    gate_bias = base[None, :] + 0.1 * jax.random.normal(ks[3], (DEPTH, MLSTM_GATES), f32)
    return {
        "x_prompt": jax.random.normal(ks[0], (BATCH, SEQ, D_MODEL), f32),
        "x_sample": jax.random.normal(ks[1], (DEC_BATCH, DEC_SEQ, D_MODEL), f32),
        "mix_pre_norm": gain(ks[2], D_MODEL),
        "w_in": w(ks[4], (DEPTH, D_MODEL, IN_COLS), D_MODEL),
        "gate_bias": gate_bias,
        "mlstm_norm": gain(ks[5], MLSTM_V_W),
        "q_norm": gain(ks[6], Q_LORA),
        "kv_norm": gain(ks[7], KV_LORA),
        "w_uq": w(ks[8], (DEPTH, Q_LORA, MLA_HEADS * (QK_NOPE + QK_ROPE)), Q_LORA),
        "w_ukv": w(ks[9], (DEPTH, KV_LORA, MLA_HEADS * (QK_NOPE + V_HEAD)), KV_LORA),
        "w_out": w(ks[10], (DEPTH, MIX_WIDTH, D_MODEL), MIX_WIDTH),
        "mix_post_norm": gain(ks[11], D_MODEL),
        "ffn_pre_norm": gain(ks[12], D_MODEL),
        "w_gate": w(ks[13], (DEPTH, D_MODEL, D_FF), D_MODEL),
        "w_up": w(ks[14], (DEPTH, D_MODEL, D_FF), D_MODEL),
        "w_down": w(ks[15], (DEPTH, D_FF, D_MODEL), D_FF),
        "ffn_post_norm": gain(ks[16], D_MODEL),
    }


def reference(x_prompt, x_sample, mix_pre_norm, w_in, gate_bias, mlstm_norm, q_norm, kv_norm,
              w_uq, w_ukv, w_out, mix_post_norm, ffn_pre_norm, w_gate, w_up, w_down, ffn_post_norm):
    def trunk(x):
        for l in range(DEPTH):
            h = rms_norm(x, mix_pre_norm[l])
            m = token_mixer(h, w_in[l], gate_bias[l], mlstm_norm[l], q_norm[l], kv_norm[l],
                            w_uq[l], w_ukv[l], w_out[l])
            x = x + rms_norm(m, mix_post_norm[l])
            h = rms_norm(x, ffn_pre_norm[l])
            f = (jax.nn.silu(h @ w_gate[l]) * (h @ w_up[l])) @ w_down[l]
            x = x + rms_norm(f, ffn_post_norm[l])
        return x

    y_prompt = trunk(x_prompt)
    y_sample = trunk(x_sample)
    return (y_prompt, y_sample)
```

```python
import functools
import math

import jax
import jax.numpy as jnp
from jax import lax
from jax.experimental import pallas as pl
from jax.experimental.pallas import tpu as pltpu

F32 = jnp.float32
BF16 = jnp.bfloat16

EPS = 1e-6
D_MODEL = 4096
MLSTM_HEADS = 4
MLSTM_QK = 256
MLSTM_V = 512
MLSTM_QK_W = MLSTM_HEADS * MLSTM_QK
MLSTM_V_W = MLSTM_HEADS * MLSTM_V
MLSTM_GATES = 4 * MLSTM_HEADS
QK_NOPE = 128
QK_ROPE = 64
V_HEAD = 128
MLA_HEADS = 16
Q_LORA = 1024
KV_LORA = 512
ROPE_THETA = 10000.0
Q_HEAD_PAD = 256
LANES = 128

P1_Q, P1_K, P1_V, P1_O = 0, 1024, 2048, 4096
P1_QDN, P1_KVDN = 6144, 7168
P1_COLS = 7680
P2_COLS = 256
P2_GATE_LANE = 128

MLSTM_CHUNK = 256
ATT_TQ = 256

VMEM_LIMIT_BYTES = 56 * 1024 * 1024


def _params(*sem):
    return pltpu.CompilerParams(dimension_semantics=sem, vmem_limit_bytes=VMEM_LIMIT_BYTES)


def _rms(x, gain):
    return x * lax.rsqrt(jnp.mean(x * x, axis=-1, keepdims=True) + EPS) * gain


def _norm_matmul_kernel(x_ref, g_ref, w_ref, o_ref, h_ref):
    @pl.when(pl.program_id(1) == 0)
    def _():
        h_ref[...] = _rms(x_ref[...].astype(F32), g_ref[...]).astype(BF16)

    o_ref[...] = jnp.dot(h_ref[...], w_ref[...], preferred_element_type=F32).astype(o_ref.dtype)


def norm_matmul(x, gain, w, *, x_col_block, tm, tn, out_dtype, name):
    m = x.shape[0]
    k, n = w.shape
    assert m % tm == 0 and n % tn == 0 and (x_col_block + 1) * k <= x.shape[1]
    return pl.pallas_call(
        _norm_matmul_kernel,
        grid=(m // tm, n // tn),
        in_specs=[
            pl.BlockSpec((tm, k), lambda i, j: (i, x_col_block)),
            pl.BlockSpec((1, k), lambda i, j: (0, 0)),
            pl.BlockSpec((k, tn), lambda i, j: (0, j)),
        ],
        out_specs=pl.BlockSpec((tm, tn), lambda i, j: (i, j)),
        out_shape=jax.ShapeDtypeStruct((m, n), out_dtype),
        scratch_shapes=[pltpu.VMEM((tm, k), BF16)],
        compiler_params=_params("parallel", "arbitrary"),
        name=name,
    )(x, gain, w)


def _log_sigmoid(x):
    return jnp.minimum(x, 0.0) - jnp.log1p(jnp.exp(-jnp.abs(x)))


def _mlstm_kernel(*refs, backward):
    if backward:
        (q_ref, k_ref, v_ref, g_ref, gb_ref, o_ref, hf_ref, nrm_ref,
         out_ref, c_ref, n_ref, m_ref) = refs
    else:
        q_ref, k_ref, v_ref, g_ref, gb_ref, out_ref, c_ref, n_ref, m_ref = refs
    L = q_ref.shape[1]

    @pl.when(pl.program_id(1) == 0)
    def _():
        c_ref[...] = jnp.zeros_like(c_ref)
        n_ref[...] = jnp.zeros_like(n_ref)
        m_ref[...] = jnp.zeros_like(m_ref)

    gates = g_ref[0][:, P2_GATE_LANE:P2_GATE_LANE + LANES] + gb_ref[...]
    gates_t = gates.T
    row = lax.broadcasted_iota(jnp.int32, (L, L), 0)
    col = lax.broadcasted_iota(jnp.int32, (L, L), 1)
    if backward:
        mask, mask_t = col >= row, row >= col
    else:
        mask, mask_t = col <= row, row <= col

    for h in range(MLSTM_HEADS):
        ii = (2 * MLSTM_HEADS if backward else 0) + h
        fi = ii + MLSTM_HEADS
        i_col, f_col = gates[:, ii:ii + 1], gates[:, fi:fi + 1]
        i_row, f_row = gates_t[ii:ii + 1, :], gates_t[fi:fi + 1, :]
        lf_col, lf_row = _log_sigmoid(f_col), _log_sigmoid(f_row)
        b_col = jnp.sum(jnp.where(mask, lf_row, 0.0), axis=1, keepdims=True)
        b_row = jnp.sum(jnp.where(mask_t, lf_col, 0.0), axis=0, keepdims=True)
        b_all = jnp.sum(lf_row, axis=1, keepdims=True)
        m_prev = m_ref[h][:, 0:1]

        dmat = jnp.where(mask, b_col - b_row + i_row, -jnp.inf)
        inter = b_col + m_prev
        mt = jnp.maximum(inter, jnp.max(dmat, axis=1, keepdims=True))
        a_inter = jnp.exp(inter - mt)
        pmat = jnp.exp(dmat - mt)

        q = q_ref[0][:, h * MLSTM_QK:(h + 1) * MLSTM_QK] * (MLSTM_QK ** -0.5)
        k = k_ref[0][:, h * MLSTM_QK:(h + 1) * MLSTM_QK]
        v = v_ref[0][:, h * MLSTM_V:(h + 1) * MLSTM_V]
        k_f32 = k.astype(F32)
        k_t = k_f32.T.astype(BF16)

        s = jnp.dot(q, k_t, preferred_element_type=F32) * pmat
        c_old = c_ref[h]
        n_old = n_ref[h]
        num = (jnp.dot(s.astype(BF16), v, preferred_element_type=F32)
               + a_inter * jnp.dot(q, c_old.astype(BF16), preferred_element_type=F32))
        den = (jnp.sum(s, axis=1, keepdims=True)
               + a_inter * jnp.sum(q.astype(F32) * n_old, axis=1, keepdims=True))
        hh = num / jnp.maximum(jnp.abs(den), jnp.exp(-mt))

        g_col = b_all - b_col + i_col
        g_row = b_all - b_row + i_row
        m_new = jnp.maximum(b_all + m_prev, jnp.max(g_row, axis=1, keepdims=True))
        decay = jnp.exp(b_all + m_prev - m_new)
        w_col = jnp.exp(g_col - m_new)
        vw = (v.astype(F32) * w_col).astype(BF16)
        c_ref[h] = decay * c_old + jnp.dot(k_t, vw, preferred_element_type=F32)
        n_ref[h] = decay * n_old + jnp.sum(w_col * k_f32, axis=0, keepdims=True)
        m_ref[h] = jnp.broadcast_to(m_new, (1, LANES))

        vs = slice(h * MLSTM_V, (h + 1) * MLSTM_V)
        if backward:
            tot = hf_ref[0][:, vs] + hh
            y = _rms(tot, nrm_ref[:, vs])
            og = jax.nn.sigmoid(o_ref[0][:, vs].astype(F32))
            out_ref[0, :, vs] = (og * y).astype(out_ref.dtype)
        else:
            out_ref[0, :, vs] = hh


def mlstm_pass(p1, p2, gate_bias_row, *, backward, h_fwd=None, norm_gain=None, name):
    b, s, _ = p1.shape
    L = MLSTM_CHUNK
    nc = s // L
    assert s % L == 0

    def cidx(c):
        return nc - 1 - c if backward else c

    qk_w, v_w = MLSTM_QK_W, MLSTM_V_W
    in_specs = [
        pl.BlockSpec((1, L, qk_w), lambda bi, c: (bi, cidx(c), P1_Q // qk_w)),
        pl.BlockSpec((1, L, qk_w), lambda bi, c: (bi, cidx(c), P1_K // qk_w)),
        pl.BlockSpec((1, L, v_w), lambda bi, c: (bi, cidx(c), P1_V // v_w)),
        pl.BlockSpec((1, L, P2_COLS), lambda bi, c: (bi, cidx(c), 0)),
        pl.BlockSpec((1, LANES), lambda bi, c: (0, 0)),
    ]
    args = [p1, p1, p1, p2, gate_bias_row]
    if backward:
        in_specs += [
            pl.BlockSpec((1, L, v_w), lambda bi, c: (bi, cidx(c), P1_O // v_w)),
            pl.BlockSpec((1, L, v_w), lambda bi, c: (bi, cidx(c), 0)),
            pl.BlockSpec((1, v_w), lambda bi, c: (0, 0)),
        ]
        args += [p1, h_fwd, norm_gain]
    out_dtype = BF16 if backward else F32
    return pl.pallas_call(
        functools.partial(_mlstm_kernel, backward=backward),
        grid=(b, nc),
        in_specs=in_specs,
        out_specs=pl.BlockSpec((1, L, v_w), lambda bi, c: (bi, cidx(c), 0)),
        out_shape=jax.ShapeDtypeStruct((b, s, v_w), out_dtype),
        scratch_shapes=[
            pltpu.VMEM((MLSTM_HEADS, MLSTM_QK, MLSTM_V), F32),
            pltpu.VMEM((MLSTM_HEADS, 1, MLSTM_QK), F32),
            pltpu.VMEM((MLSTM_HEADS, 1, LANES), F32),
        ],
        compiler_params=_params("parallel", "arbitrary"),
        name=name,
    )(*args)


def _rotate_half(x):
    lane = lax.broadcasted_iota(jnp.int32, x.shape, 1)
    return jnp.where(lane < QK_ROPE // 2, pltpu.roll(x, LANES - QK_ROPE // 2, 1),
                     pltpu.roll(x, QK_ROPE // 2, 1))


def _attention_kernel(q_ref, kn_ref, v_ref, kr_ref, cos_ref, sin_ref, o_ref, k_scr, *, exp_scale):
    qi = pl.program_id(2)
    tq = q_ref.shape[1]

    @pl.when(qi == 0)
    def _():
        kr = kr_ref[0][:, 0:LANES]
        kr = kr * cos_ref[...] + _rotate_half(kr) * sin_ref[...]
        k_scr[:, 0:QK_NOPE] = kn_ref[0]
        k_scr[:, QK_NOPE:Q_HEAD_PAD] = kr.astype(BF16)

    q = q_ref[0]
    rows = pl.ds(pl.multiple_of(qi * tq, tq), tq)
    qr = q[:, QK_NOPE:Q_HEAD_PAD].astype(F32)
    qr = qr * cos_ref[rows, :] + _rotate_half(qr) * sin_ref[rows, :]
    qfull = jnp.concatenate([q[:, 0:QK_NOPE], qr.astype(BF16)], axis=1)
    s = lax.dot_general(qfull, k_scr[...], (((1,), (1,)), ((), ())),
                        preferred_element_type=F32)
    m = jnp.max(s, axis=1, keepdims=True)
    p = jnp.exp2((s - m) * exp_scale)
    l = jnp.sum(p, axis=1, keepdims=True)
    o = jnp.dot(p.astype(BF16), v_ref[0], preferred_element_type=F32)
    o_ref[0] = (o / l).astype(o_ref.dtype)


def mla_attention(qf, kvf, p2, cos_t, sin_t, *, name):
    b, s, _ = qf.shape
    tq = ATT_TQ
    assert s % tq == 0
    exp_scale = (QK_NOPE + QK_ROPE) ** -0.5 * math.log2(math.e)
    return pl.pallas_call(
        functools.partial(_attention_kernel, exp_scale=exp_scale),
        grid=(b, MLA_HEADS, s // tq),
        in_specs=[
            pl.BlockSpec((1, tq, Q_HEAD_PAD), lambda bi, h, qi: (bi, qi, h)),
            pl.BlockSpec((1, s, QK_NOPE), lambda bi, h, qi: (bi, 0, 2 * h)),
            pl.BlockSpec((1, s, V_HEAD), lambda bi, h, qi: (bi, 0, 2 * h + 1)),
            pl.BlockSpec((1, s, P2_COLS), lambda bi, h, qi: (bi, 0, 0)),
            pl.BlockSpec((s, LANES), lambda bi, h, qi: (0, 0)),
            pl.BlockSpec((s, LANES), lambda bi, h, qi: (0, 0)),
        ],
        out_specs=pl.BlockSpec((1, tq, V_HEAD), lambda bi, h, qi: (bi, qi, h)),
        out_shape=jax.ShapeDtypeStruct((b, s, MLA_HEADS * V_HEAD), BF16),
        scratch_shapes=[pltpu.VMEM((s, Q_HEAD_PAD), BF16)],
        compiler_params=_params("parallel", "parallel", "arbitrary"),
        name=name,
    )(qf, kvf, kvf, p2, cos_t, sin_t)


def _outproj_kernel(a_ref, b_ref, w1_ref, w2_ref, x_ref, g_ref, o_ref):
    j = pl.program_id(1)
    tn = w1_ref.shape[1]
    acc = (jnp.dot(a_ref[...], w1_ref[...], preferred_element_type=F32)
           + jnp.dot(b_ref[...], w2_ref[...], preferred_element_type=F32))
    o_ref[:, pl.ds(pl.multiple_of(j * tn, tn), tn)] = acc

    @pl.when(j == pl.num_programs(1) - 1)
    def _():
        o_ref[...] = x_ref[...] + _rms(o_ref[...], g_ref[...])


def outproj(ya, yb, w, x, gain, *, tm, tn, name):
    m, ka = ya.shape
    n = w.shape[1]
    assert w.shape[0] == 2 * ka and yb.shape == ya.shape and m % tm == 0 and n % tn == 0
    return pl.pallas_call(
        _outproj_kernel,
        grid=(m // tm, n // tn),
        in_specs=[
            pl.BlockSpec((tm, ka), lambda i, j: (i, 0)),
            pl.BlockSpec((tm, ka), lambda i, j: (i, 0)),
            pl.BlockSpec((ka, tn), lambda i, j: (0, j)),
            pl.BlockSpec((ka, tn), lambda i, j: (1, j)),
            pl.BlockSpec((tm, n), lambda i, j: (i, 0)),
            pl.BlockSpec((1, n), lambda i, j: (0, 0)),
        ],
        out_specs=pl.BlockSpec((tm, n), lambda i, j: (i, 0)),
        out_shape=jax.ShapeDtypeStruct((m, n), F32),
        compiler_params=_params("parallel", "arbitrary"),
        name=name,
    )(ya, yb, w, w, x, gain)


def _ffn_kernel(x_ref, gpre_ref, wg_ref, wu_ref, wd_ref, gpost_ref, o_ref, h_ref, *, n_chunk):
    j = pl.program_id(1)

    @pl.when(j == 0)
    def _():
        h_ref[...] = _rms(x_ref[...], gpre_ref[...]).astype(BF16)

    h = h_ref[...]
    g = jnp.dot(h, wg_ref[...], preferred_element_type=F32)
    u = jnp.dot(h, wu_ref[...], preferred_element_type=F32)
    a = (g * jax.nn.sigmoid(g) * u).astype(BF16)
    n = o_ref.shape[1]

    @pl.when(j == 0)
    def _():
        for n0 in range(0, n, n_chunk):
            o_ref[:, n0:n0 + n_chunk] = jnp.dot(a, wd_ref[:, n0:n0 + n_chunk],
                                                preferred_element_type=F32)

    @pl.when(j > 0)
    def _():
        for n0 in range(0, n, n_chunk):
            o_ref[:, n0:n0 + n_chunk] += jnp.dot(a, wd_ref[:, n0:n0 + n_chunk],
                                                 preferred_element_type=F32)

    @pl.when(j == pl.num_programs(1) - 1)
    def _():
        o_ref[...] = x_ref[...] + _rms(o_ref[...], gpost_ref[...])


def ffn(x, gpre, wg, wu, wd, gpost, *, tm, tf, name):
    m, d = x.shape
    f = wg.shape[1]
    assert m % tm == 0 and f % tf == 0
    return pl.pallas_call(
        functools.partial(_ffn_kernel, n_chunk=1024),
        grid=(m // tm, f // tf),
        in_specs=[
            pl.BlockSpec((tm, d), lambda i, j: (i, 0), pipeline_mode=pl.Buffered(1)),
            pl.BlockSpec((1, d), lambda i, j: (0, 0)),
            pl.BlockSpec((d, tf), lambda i, j: (0, j)),
            pl.BlockSpec((d, tf), lambda i, j: (0, j)),
            pl.BlockSpec((tf, d), lambda i, j: (j, 0)),
            pl.BlockSpec((1, d), lambda i, j: (0, 0)),
        ],
        out_specs=pl.BlockSpec((tm, d), lambda i, j: (i, 0)),
        out_shape=jax.ShapeDtypeStruct((m, d), F32),
        scratch_shapes=[pltpu.VMEM((tm, d), BF16)],
        compiler_params=_params("parallel", "arbitrary"),
        name=name,
    )(x, gpre, wg, wu, wd, gpost)


def _rope_tables(s):
    inv = 1.0 / (ROPE_THETA ** (jnp.arange(0, QK_ROPE, 2, dtype=F32) / QK_ROPE))
    ang = jnp.arange(s, dtype=F32)[:, None] * inv[None, :]
    cos, sin = jnp.cos(ang), jnp.sin(ang)
    pad = LANES - QK_ROPE
    cos_t = jnp.concatenate([cos, cos, jnp.ones((s, pad), F32)], axis=1)
    sin_t = jnp.concatenate([-sin, sin, jnp.zeros((s, pad), F32)], axis=1)
    return cos_t, sin_t


def _pack_weights(w_in, gate_bias, w_uq, w_ukv, w_out, w_gate, w_up, w_down):
    gate0 = 2 * MLSTM_QK_W + 2 * MLSTM_V_W
    qdn0 = gate0 + MLSTM_GATES
    kr0 = qdn0 + Q_LORA + KV_LORA
    w1 = jnp.concatenate([w_in[:, :gate0], w_in[:, qdn0:kr0]], axis=1).astype(BF16)
    w2 = jnp.zeros((D_MODEL, P2_COLS), F32)
    w2 = w2.at[:, 0:QK_ROPE].set(w_in[:, kr0:kr0 + QK_ROPE])
    w2 = w2.at[:, P2_GATE_LANE:P2_GATE_LANE + MLSTM_GATES].set(w_in[:, gate0:qdn0]).astype(BF16)
    gb = jnp.zeros((1, LANES), F32).at[0, 0:MLSTM_GATES].set(gate_bias)
    wq = w_uq.reshape(Q_LORA, MLA_HEADS, QK_NOPE + QK_ROPE)
    wq = jnp.pad(wq, ((0, 0), (0, 0), (0, Q_HEAD_PAD - QK_NOPE - QK_ROPE)))
    wq = wq.reshape(Q_LORA, MLA_HEADS * Q_HEAD_PAD).astype(BF16)
    return dict(w1=w1, w2=w2, gb=gb, wq=wq, wkv=w_ukv.astype(BF16), wo=w_out.astype(BF16),
                wg=w_gate.astype(BF16), wu=w_up.astype(BF16), wd=w_down.astype(BF16))


def _trunk(x, pw, gains, tag):
    b, s, d = x.shape
    m = b * s
    x2 = x.reshape(m, d)
    p1 = norm_matmul(x2, gains["mix_pre"], pw["w1"], x_col_block=0, tm=512, tn=1280,
                     out_dtype=BF16, name=f"inproj_main_{tag}")
    p2 = norm_matmul(x2, gains["mix_pre"], pw["w2"], x_col_block=0, tm=512, tn=P2_COLS,
                     out_dtype=F32, name=f"inproj_small_{tag}")
    p1b, p2b = p1.reshape(b, s, P1_COLS), p2.reshape(b, s, P2_COLS)

    h_fwd = mlstm_pass(p1b, p2b, pw["gb"], backward=False, name=f"mlstm_fwd_{tag}")
    y_mlstm = mlstm_pass(p1b, p2b, pw["gb"], backward=True, h_fwd=h_fwd,
                         norm_gain=gains["mlstm"], name=f"mlstm_bwd_{tag}")

    qf = norm_matmul(p1, gains["q"], pw["wq"], x_col_block=P1_QDN // Q_LORA, tm=512, tn=2048,
                     out_dtype=BF16, name=f"q_up_{tag}")
    kvf = norm_matmul(p1, gains["kv"], pw["wkv"], x_col_block=P1_KVDN // KV_LORA, tm=512, tn=2048,
                      out_dtype=BF16, name=f"kv_up_{tag}")
    cos_t, sin_t = _rope_tables(s)
    y_mla = mla_attention(qf.reshape(b, s, -1), kvf.reshape(b, s, -1), p2b, cos_t, sin_t,
                          name=f"mla_attn_{tag}")

    x1 = outproj(y_mlstm.reshape(m, -1), y_mla.reshape(m, -1), pw["wo"], x2, gains["mix_post"],
                 tm=512, tn=512, name=f"outproj_{tag}")
    y = ffn(x1, gains["ffn_pre"], pw["wg"], pw["wu"], pw["wd"], gains["ffn_post"],
            tm=512, tf=256, name=f"ffn_{tag}")
    return y.reshape(b, s, d)


def kernel(x_prompt, x_sample, mix_pre_norm, w_in, gate_bias, mlstm_norm, q_norm, kv_norm, w_uq, w_ukv, w_out, mix_post_norm, ffn_pre_norm, w_gate, w_up, w_down, ffn_post_norm):
    assert w_in.shape[0] == 1, "single-layer trunk"
    pw = _pack_weights(w_in[0], gate_bias[0], w_uq[0], w_ukv[0], w_out[0], w_gate[0], w_up[0], w_down[0])
    gains = dict(mix_pre=mix_pre_norm, mlstm=mlstm_norm, q=q_norm, kv=kv_norm,
                 mix_post=mix_post_norm, ffn_pre=ffn_pre_norm, ffn_post=ffn_post_norm)
    gains = {k_: v_.astype(F32).reshape(1, -1) for k_, v_ in gains.items()}
    y_prompt = _trunk(x_prompt, pw, gains, "prompt")
    y_sample = _trunk(x_sample, pw, gains, "sample")
    return (y_prompt, y_sample)
```

```python
import functools
import math

import jax
import jax.numpy as jnp
from jax import lax
from jax.experimental import pallas as pl
from jax.experimental.pallas import tpu as pltpu

F32 = jnp.float32
BF16 = jnp.bfloat16

EPS = 1e-6
D_MODEL = 4096
MLSTM_HEADS = 4
MLSTM_QK = 256
MLSTM_V = 512
MLSTM_QK_W = MLSTM_HEADS * MLSTM_QK
MLSTM_V_W = MLSTM_HEADS * MLSTM_V
MLSTM_GATES = 4 * MLSTM_HEADS
QK_NOPE = 128
QK_ROPE = 64
V_HEAD = 128
MLA_HEADS = 16
Q_LORA = 1024
KV_LORA = 512
ROPE_THETA = 10000.0
Q_HEAD_PAD = 256
LANES = 128

P1_Q, P1_K, P1_V, P1_O = 0, 1024, 2048, 4096
P1_QDN, P1_KVDN = 6144, 7168
P1_COLS = 7680
P2_COLS = 256
P2_GATE_LANE = 128

MLSTM_CHUNK = 256
ATT_TQ = 512
ATT_Q_SUB = 256

VMEM_LIMIT_BYTES = 56 * 1024 * 1024


def _params(*sem):
    return pltpu.CompilerParams(dimension_semantics=sem, vmem_limit_bytes=VMEM_LIMIT_BYTES)


def _rms(x, gain):
    return x * lax.rsqrt(jnp.mean(x * x, axis=-1, keepdims=True) + EPS) * gain


def _norm_matmul_kernel(x_ref, g_ref, w_ref, o_ref, h_ref):
    @pl.when(pl.program_id(1) == 0)
    def _():
        h_ref[...] = _rms(x_ref[...].astype(F32), g_ref[...]).astype(BF16)

    o_ref[...] = jnp.dot(h_ref[...], w_ref[...], preferred_element_type=F32).astype(o_ref.dtype)


def norm_matmul(x, gain, w, *, x_col_block, tm, tn, out_dtype, name):
    m = x.shape[0]
    k, n = w.shape
    assert m % tm == 0 and n % tn == 0 and (x_col_block + 1) * k <= x.shape[1]
    return pl.pallas_call(
        _norm_matmul_kernel,
        grid=(m // tm, n // tn),
        in_specs=[
            pl.BlockSpec((tm, k), lambda i, j: (i, x_col_block)),
            pl.BlockSpec((1, k), lambda i, j: (0, 0)),
            pl.BlockSpec((k, tn), lambda i, j: (0, j)),
        ],
        out_specs=pl.BlockSpec((tm, tn), lambda i, j: (i, j)),
        out_shape=jax.ShapeDtypeStruct((m, n), out_dtype),
        scratch_shapes=[pltpu.VMEM((tm, k), BF16)],
        compiler_params=_params("parallel", "arbitrary"),
        name=name,
    )(x, gain, w)


def _inproj_kernel(x_ref, g_ref, w_ref, w2_ref, o_ref, o2_ref, h_ref):
    @pl.when(pl.program_id(1) == 0)
    def _():
        h_ref[...] = _rms(x_ref[...], g_ref[...]).astype(BF16)
        o2_ref[...] = jnp.dot(h_ref[...], w2_ref[...], preferred_element_type=F32)

    o_ref[...] = jnp.dot(h_ref[...], w_ref[...], preferred_element_type=F32).astype(o_ref.dtype)


def inproj(x, gain, w, w2, *, tm, tn, name):
    m, k = x.shape
    n, n2 = w.shape[1], w2.shape[1]
    assert m % tm == 0 and n % tn == 0
    return pl.pallas_call(
        _inproj_kernel,
        grid=(m // tm, n // tn),
        in_specs=[
            pl.BlockSpec((tm, k), lambda i, j: (i, 0)),
            pl.BlockSpec((1, k), lambda i, j: (0, 0)),
            pl.BlockSpec((k, tn), lambda i, j: (0, j)),
            pl.BlockSpec((k, n2), lambda i, j: (0, 0)),
        ],
        out_specs=[pl.BlockSpec((tm, tn), lambda i, j: (i, j)),
                   pl.BlockSpec((tm, n2), lambda i, j: (i, 0))],
        out_shape=[jax.ShapeDtypeStruct((m, n), BF16), jax.ShapeDtypeStruct((m, n2), F32)],
        scratch_shapes=[pltpu.VMEM((tm, k), BF16)],
        compiler_params=_params("parallel", "arbitrary"),
        name=name,
    )(x, gain, w, w2)


def _log_sigmoid(x):
    return jnp.minimum(x, 0.0) - jnp.log1p(jnp.exp(-jnp.abs(x)))


def _mlstm_kernel(*refs, backward):
    if backward:
        (q_ref, k_ref, v_ref, g_ref, gb_ref, o_ref, hf_ref, nrm_ref,
         out_ref, c_ref, n_ref, m_ref) = refs
    else:
        q_ref, k_ref, v_ref, g_ref, gb_ref, out_ref, c_ref, n_ref, m_ref = refs
    L = q_ref.shape[1]

    @pl.when(pl.program_id(1) == 0)
    def _():
        c_ref[...] = jnp.zeros_like(c_ref)
        n_ref[...] = jnp.zeros_like(n_ref)
        m_ref[...] = jnp.zeros_like(m_ref)

    gates = g_ref[0][:, P2_GATE_LANE:P2_GATE_LANE + LANES] + gb_ref[...]
    gates_t = gates.T
    row = lax.broadcasted_iota(jnp.int32, (L, L), 0)
    col = lax.broadcasted_iota(jnp.int32, (L, L), 1)
    if backward:
        mask, mask_t = col >= row, row >= col
    else:
        mask, mask_t = col <= row, row <= col

    for h in range(MLSTM_HEADS):
        ii = (2 * MLSTM_HEADS if backward else 0) + h
        fi = ii + MLSTM_HEADS
        i_col, f_col = gates[:, ii:ii + 1], gates[:, fi:fi + 1]
        i_row, f_row = gates_t[ii:ii + 1, :], gates_t[fi:fi + 1, :]
        lf_col, lf_row = _log_sigmoid(f_col), _log_sigmoid(f_row)
        b_col = jnp.sum(jnp.where(mask, lf_row, 0.0), axis=1, keepdims=True)
        b_row = jnp.sum(jnp.where(mask_t, lf_col, 0.0), axis=0, keepdims=True)
        b_all = jnp.sum(lf_row, axis=1, keepdims=True)
        m_prev = m_ref[h][:, 0:1]

        dmat = jnp.where(mask, b_col - b_row + i_row, -jnp.inf)
        inter = b_col + m_prev
        mt = jnp.maximum(inter, jnp.max(dmat, axis=1, keepdims=True))
        a_inter = jnp.exp(inter - mt)
        pmat = jnp.exp(dmat - mt)

        q = q_ref[0][:, h * MLSTM_QK:(h + 1) * MLSTM_QK] * (MLSTM_QK ** -0.5)
        k = k_ref[0][:, h * MLSTM_QK:(h + 1) * MLSTM_QK]
        v = v_ref[0][:, h * MLSTM_V:(h + 1) * MLSTM_V]
        k_f32 = k.astype(F32)
        k_t = k_f32.T.astype(BF16)

        s = jnp.dot(q, k_t, preferred_element_type=F32) * pmat
        c_old = c_ref[h]
        n_old = n_ref[h]
        num = (jnp.dot(s.astype(BF16), v, preferred_element_type=F32)
               + a_inter * jnp.dot(q, c_old.astype(BF16), preferred_element_type=F32))
        den = (jnp.sum(s, axis=1, keepdims=True)
               + a_inter * jnp.sum(q.astype(F32) * n_old, axis=1, keepdims=True))
        hh = num / jnp.maximum(jnp.abs(den), jnp.exp(-mt))

        g_col = b_all - b_col + i_col
        g_row = b_all - b_row + i_row
        m_new = jnp.maximum(b_all + m_prev, jnp.max(g_row, axis=1, keepdims=True))
        decay = jnp.exp(b_all + m_prev - m_new)
        w_col = jnp.exp(g_col - m_new)
        vw = (v.astype(F32) * w_col).astype(BF16)
        c_ref[h] = decay * c_old + jnp.dot(k_t, vw, preferred_element_type=F32)
        n_ref[h] = decay * n_old + jnp.sum(w_col * k_f32, axis=0, keepdims=True)
        m_ref[h] = jnp.broadcast_to(m_new, (1, LANES))

        vs = slice(h * MLSTM_V, (h + 1) * MLSTM_V)
        if backward:
            tot = hf_ref[0][:, vs] + hh
            y = _rms(tot, nrm_ref[:, vs])
            og = jax.nn.sigmoid(o_ref[0][:, vs].astype(F32))
            out_ref[0, :, vs] = (og * y).astype(out_ref.dtype)
        else:
            out_ref[0, :, vs] = hh


def mlstm_pass(p1, p2, gate_bias_row, *, backward, h_fwd=None, norm_gain=None, name):
    b, s, _ = p1.shape
    L = MLSTM_CHUNK
    nc = s // L
    assert s % L == 0

    def cidx(c):
        return nc - 1 - c if backward else c

    qk_w, v_w = MLSTM_QK_W, MLSTM_V_W
    in_specs = [
        pl.BlockSpec((1, L, qk_w), lambda bi, c: (bi, cidx(c), P1_Q // qk_w)),
        pl.BlockSpec((1, L, qk_w), lambda bi, c: (bi, cidx(c), P1_K // qk_w)),
        pl.BlockSpec((1, L, v_w), lambda bi, c: (bi, cidx(c), P1_V // v_w)),
        pl.BlockSpec((1, L, P2_COLS), lambda bi, c: (bi, cidx(c), 0)),
        pl.BlockSpec((1, LANES), lambda bi, c: (0, 0)),
    ]
    args = [p1, p1, p1, p2, gate_bias_row]
    if backward:
        in_specs += [
            pl.BlockSpec((1, L, v_w), lambda bi, c: (bi, cidx(c), P1_O // v_w)),
            pl.BlockSpec((1, L, v_w), lambda bi, c: (bi, cidx(c), 0)),
            pl.BlockSpec((1, v_w), lambda bi, c: (0, 0)),
        ]
        args += [p1, h_fwd, norm_gain]
    out_dtype = BF16 if backward else F32
    return pl.pallas_call(
        functools.partial(_mlstm_kernel, backward=backward),
        grid=(b, nc),
        in_specs=in_specs,
        out_specs=pl.BlockSpec((1, L, v_w), lambda bi, c: (bi, cidx(c), 0)),
        out_shape=jax.ShapeDtypeStruct((b, s, v_w), out_dtype),
        scratch_shapes=[
            pltpu.VMEM((MLSTM_HEADS, MLSTM_QK, MLSTM_V), F32),
            pltpu.VMEM((MLSTM_HEADS, 1, MLSTM_QK), F32),
            pltpu.VMEM((MLSTM_HEADS, 1, LANES), F32),
        ],
        compiler_params=_params("parallel", "arbitrary"),
        name=name,
    )(*args)


def _rotate_half(x):
    lane = lax.broadcasted_iota(jnp.int32, x.shape, 1)
    return jnp.where(lane < QK_ROPE // 2, pltpu.roll(x, LANES - QK_ROPE // 2, 1),
                     pltpu.roll(x, QK_ROPE // 2, 1))


def _attention_kernel(q_ref, kn_ref, v_ref, kr_ref, cos_ref, sin_ref, o_ref, k_scr, v_scr, *,
                      exp_scale, q_sub):
    qi = pl.program_id(2)
    tq = q_ref.shape[1]

    @pl.when(qi == 0)
    def _():
        kr = kr_ref[0][:, 0:LANES]
        kr = kr * cos_ref[...] + _rotate_half(kr) * sin_ref[...]
        k_scr[:, 0:QK_NOPE] = kn_ref[0]
        k_scr[:, QK_NOPE:Q_HEAD_PAD] = kr.astype(BF16)
        v_scr[:, 0:V_HEAD] = v_ref[0]
        v_scr[:, V_HEAD:2 * V_HEAD] = jnp.ones((v_scr.shape[0], V_HEAD), BF16)

    for r0 in range(0, tq, q_sub):
        q = q_ref[0, r0:r0 + q_sub, :]
        rows = pl.ds(pl.multiple_of(qi * tq + r0, q_sub), q_sub)
        qr = q[:, QK_NOPE:Q_HEAD_PAD].astype(F32)
        qr = qr * cos_ref[rows, :] + _rotate_half(qr) * sin_ref[rows, :]
        qfull = jnp.concatenate([q[:, 0:QK_NOPE], qr.astype(BF16)], axis=1)
        s = lax.dot_general(qfull, k_scr[...], (((1,), (1,)), ((), ())),
                            preferred_element_type=F32)
        m = jnp.max(s, axis=1, keepdims=True)
        p = jnp.exp2((s - m) * exp_scale).astype(BF16)
        ov = jnp.dot(p, v_scr[...], preferred_element_type=F32)
        o_ref[0, r0:r0 + q_sub, :] = (ov[:, 0:V_HEAD] / ov[:, V_HEAD:V_HEAD + 1]).astype(o_ref.dtype)


def mla_attention(qf, kvf, p2, cos_t, sin_t, *, name):
    b, s, _ = qf.shape
    tq = ATT_TQ
    assert s % tq == 0
    exp_scale = (QK_NOPE + QK_ROPE) ** -0.5 * math.log2(math.e)
    return pl.pallas_call(
        functools.partial(_attention_kernel, exp_scale=exp_scale, q_sub=ATT_Q_SUB),
        grid=(b, MLA_HEADS, s // tq),
        in_specs=[
            pl.BlockSpec((1, tq, Q_HEAD_PAD), lambda bi, h, qi: (bi, qi, h)),
            pl.BlockSpec((1, s, QK_NOPE), lambda bi, h, qi: (bi, 0, 2 * h)),
            pl.BlockSpec((1, s, V_HEAD), lambda bi, h, qi: (bi, 0, 2 * h + 1)),
            pl.BlockSpec((1, s, P2_COLS), lambda bi, h, qi: (bi, 0, 0)),
            pl.BlockSpec((s, LANES), lambda bi, h, qi: (0, 0)),
            pl.BlockSpec((s, LANES), lambda bi, h, qi: (0, 0)),
        ],
        out_specs=pl.BlockSpec((1, tq, V_HEAD), lambda bi, h, qi: (bi, qi, h)),
        out_shape=jax.ShapeDtypeStruct((b, s, MLA_HEADS * V_HEAD), BF16),
        scratch_shapes=[pltpu.VMEM((s, Q_HEAD_PAD), BF16), pltpu.VMEM((s, 2 * V_HEAD), BF16)],
        compiler_params=_params("parallel", "parallel", "arbitrary"),
        name=name,
    )(qf, kvf, kvf, p2, cos_t, sin_t)


def _outproj_kernel(a_ref, b_ref, w1_ref, w2_ref, x_ref, g_ref, o_ref):
    j = pl.program_id(1)
    tn = w1_ref.shape[1]
    acc = (jnp.dot(a_ref[...], w1_ref[...], preferred_element_type=F32)
           + jnp.dot(b_ref[...], w2_ref[...], preferred_element_type=F32))
    o_ref[:, pl.ds(pl.multiple_of(j * tn, tn), tn)] = acc

    @pl.when(j == pl.num_programs(1) - 1)
    def _():
        o_ref[...] = x_ref[...] + _rms(o_ref[...], g_ref[...])


def outproj(ya, yb, w, x, gain, *, tm, tn, name):
    m, ka = ya.shape
    n = w.shape[1]
    assert w.shape[0] == 2 * ka and yb.shape == ya.shape and m % tm == 0 and n % tn == 0
    return pl.pallas_call(
        _outproj_kernel,
        grid=(m // tm, n // tn),
        in_specs=[
            pl.BlockSpec((tm, ka), lambda i, j: (i, 0)),
            pl.BlockSpec((tm, ka), lambda i, j: (i, 0)),
            pl.BlockSpec((ka, tn), lambda i, j: (0, j)),
            pl.BlockSpec((ka, tn), lambda i, j: (1, j)),
            pl.BlockSpec((tm, n), lambda i, j: (i, 0)),
            pl.BlockSpec((1, n), lambda i, j: (0, 0)),
        ],
        out_specs=pl.BlockSpec((tm, n), lambda i, j: (i, 0)),
        out_shape=jax.ShapeDtypeStruct((m, n), F32),
        compiler_params=_params("parallel", "arbitrary"),
        name=name,
    )(ya, yb, w, w, x, gain)


def _ffn_kernel(x_ref, gpre_ref, wg_ref, wu_ref, wd_ref, gpost_ref, o_ref, h_ref, *,
                f_sub, f_tail, n_chunk):
    j = pl.program_id(1)

    @pl.when(j == 0)
    def _():
        h_ref[...] = _rms(x_ref[...], gpre_ref[...]).astype(BF16)
        o_ref[...] = jnp.zeros_like(o_ref)

    def accumulate(width):
        h = h_ref[...]
        parts = []
        for c0 in range(0, width, f_sub):
            g = jnp.dot(h, wg_ref[:, c0:c0 + f_sub], preferred_element_type=F32)
            u = jnp.dot(h, wu_ref[:, c0:c0 + f_sub], preferred_element_type=F32)
            parts.append((g * jax.nn.sigmoid(g) * u).astype(BF16))
        a = parts[0] if len(parts) == 1 else jnp.concatenate(parts, axis=1)
        for n0 in range(0, o_ref.shape[1], n_chunk):
            o_ref[:, n0:n0 + n_chunk] += jnp.dot(a, wd_ref[0:width, n0:n0 + n_chunk],
                                                 preferred_element_type=F32)

    last = pl.num_programs(1) - 1
    if f_tail == wg_ref.shape[1]:
        accumulate(f_tail)
    else:
        pl.when(j < last)(lambda: accumulate(wg_ref.shape[1]))
        pl.when(j == last)(lambda: accumulate(f_tail))

    @pl.when(j == last)
    def _():
        o_ref[...] = x_ref[...] + _rms(o_ref[...], gpost_ref[...])


def ffn(x, gpre, wg, wu, wd, gpost, *, tm, tf, name):
    m, d = x.shape
    f = wg.shape[1]
    f_sub = 256
    f_tail = f - (pl.cdiv(f, tf) - 1) * tf
    assert m % tm == 0 and tf % f_sub == 0 and f_tail % f_sub == 0
    return pl.pallas_call(
        functools.partial(_ffn_kernel, f_sub=f_sub, f_tail=f_tail, n_chunk=1024),
        grid=(m // tm, pl.cdiv(f, tf)),
        in_specs=[
            pl.BlockSpec((tm, d), lambda i, j: (i, 0), pipeline_mode=pl.Buffered(1)),
            pl.BlockSpec((1, d), lambda i, j: (0, 0)),
            pl.BlockSpec((d, tf), lambda i, j: (0, j)),
            pl.BlockSpec((d, tf), lambda i, j: (0, j)),
            pl.BlockSpec((tf, d), lambda i, j: (j, 0)),
            pl.BlockSpec((1, d), lambda i, j: (0, 0)),
        ],
        out_specs=pl.BlockSpec((tm, d), lambda i, j: (i, 0), pipeline_mode=pl.Buffered(1)),
        out_shape=jax.ShapeDtypeStruct((m, d), F32),
        scratch_shapes=[pltpu.VMEM((tm, d), BF16)],
        compiler_params=_params("parallel", "arbitrary"),
        name=name,
    )(x, gpre, wg, wu, wd, gpost)


def _rope_tables(s):
    inv = 1.0 / (ROPE_THETA ** (jnp.arange(0, QK_ROPE, 2, dtype=F32) / QK_ROPE))
    ang = jnp.arange(s, dtype=F32)[:, None] * inv[None, :]
    cos, sin = jnp.cos(ang), jnp.sin(ang)
    pad = LANES - QK_ROPE
    cos_t = jnp.concatenate([cos, cos, jnp.ones((s, pad), F32)], axis=1)
    sin_t = jnp.concatenate([-sin, sin, jnp.zeros((s, pad), F32)], axis=1)
    return cos_t, sin_t


def _pack_weights(w_in, gate_bias, w_uq, w_ukv, w_out, w_gate, w_up, w_down):
    gate0 = 2 * MLSTM_QK_W + 2 * MLSTM_V_W
    qdn0 = gate0 + MLSTM_GATES
    kr0 = qdn0 + Q_LORA + KV_LORA
    w1 = jnp.concatenate([w_in[:, :gate0], w_in[:, qdn0:kr0]], axis=1).astype(BF16)
    w2 = jnp.zeros((D_MODEL, P2_COLS), F32)
    w2 = w2.at[:, 0:QK_ROPE].set(w_in[:, kr0:kr0 + QK_ROPE])
    w2 = w2.at[:, P2_GATE_LANE:P2_GATE_LANE + MLSTM_GATES].set(w_in[:, gate0:qdn0]).astype(BF16)
    gb = jnp.zeros((1, LANES), F32).at[0, 0:MLSTM_GATES].set(gate_bias)
    wq = w_uq.reshape(Q_LORA, MLA_HEADS, QK_NOPE + QK_ROPE)
    wq = jnp.pad(wq, ((0, 0), (0, 0), (0, Q_HEAD_PAD - QK_NOPE - QK_ROPE)))
    wq = wq.reshape(Q_LORA, MLA_HEADS * Q_HEAD_PAD).astype(BF16)
    return dict(w1=w1, w2=w2, gb=gb, wq=wq, wkv=w_ukv.astype(BF16), wo=w_out.astype(BF16),
                wg=w_gate.astype(BF16), wu=w_up.astype(BF16), wd=w_down.astype(BF16))


def _trunk(x, pw, gains, tag):
    b, s, d = x.shape
    m = b * s
    x2 = x.reshape(m, d)
    p1, p2 = inproj(x2, gains["mix_pre"], pw["w1"], pw["w2"], tm=512, tn=1280, name=f"inproj_{tag}")
    p1b, p2b = p1.reshape(b, s, P1_COLS), p2.reshape(b, s, P2_COLS)

    h_fwd = mlstm_pass(p1b, p2b, pw["gb"], backward=False, name=f"mlstm_fwd_{tag}")
    y_mlstm = mlstm_pass(p1b, p2b, pw["gb"], backward=True, h_fwd=h_fwd,
                         norm_gain=gains["mlstm"], name=f"mlstm_bwd_{tag}")

    qf = norm_matmul(p1, gains["q"], pw["wq"], x_col_block=P1_QDN // Q_LORA, tm=512, tn=2048,
                     out_dtype=BF16, name=f"q_up_{tag}")
    kvf = norm_matmul(p1, gains["kv"], pw["wkv"], x_col_block=P1_KVDN // KV_LORA, tm=512, tn=2048,
                      out_dtype=BF16, name=f"kv_up_{tag}")
    cos_t, sin_t = _rope_tables(s)
    y_mla = mla_attention(qf.reshape(b, s, -1), kvf.reshape(b, s, -1), p2b, cos_t, sin_t,
                          name=f"mla_attn_{tag}")

    x1 = outproj(y_mlstm.reshape(m, -1), y_mla.reshape(m, -1), pw["wo"], x2, gains["mix_post"],
                 tm=512, tn=512, name=f"outproj_{tag}")
    y = ffn(x1, gains["ffn_pre"], pw["wg"], pw["wu"], pw["wd"], gains["ffn_post"],
            tm=512, tf=512, name=f"ffn_{tag}")
    return y.reshape(b, s, d)


def kernel(x_prompt, x_sample, mix_pre_norm, w_in, gate_bias, mlstm_norm, q_norm, kv_norm, w_uq, w_ukv, w_out, mix_post_norm, ffn_pre_norm, w_gate, w_up, w_down, ffn_post_norm):
    assert w_in.shape[0] == 1, "single-layer trunk"
    pw = _pack_weights(w_in[0], gate_bias[0], w_uq[0], w_ukv[0], w_out[0], w_gate[0], w_up[0], w_down[0])
    gains = dict(mix_pre=mix_pre_norm, mlstm=mlstm_norm, q=q_norm, kv=kv_norm,
                 mix_post=mix_post_norm, ffn_pre=ffn_pre_norm, ffn_post=ffn_post_norm)
    gains = {k_: v_.astype(F32).reshape(1, -1) for k_, v_ in gains.items()}
    y_prompt = _trunk(x_prompt, pw, gains, "prompt")
    y_sample = _trunk(x_sample, pw, gains, "sample")
    return (y_prompt, y_sample)
```
